```python
import math
import jax, jax.numpy as jnp
from jax import lax
import numpy as np

D_MODEL = 1024
BATCH = 16
SEQ = 4096
DEPTH = 1
DEC_BATCH = 16
DEC_SEQ = 32
PAST_LEN = 2048

CHUNK = 64
SSM_EXPAND = 2
D_INNER = SSM_EXPAND * D_MODEL
SSM_HEAD_DIM = 64
SSM_HEADS = D_INNER // SSM_HEAD_DIM
SSM_GROUPS = 8
SSM_HPG = SSM_HEADS // SSM_GROUPS
D_STATE = 128
SSM_CONV = 4
CONV_DIM = D_INNER + 2 * SSM_GROUPS * D_STATE
HEAD_DIM = 64
N_HEADS = D_MODEL // HEAD_DIM
N_KV_HEADS = 4
Q_PER_KV = N_HEADS // N_KV_HEADS
WINDOW = 128
PREV_CHUNKS = WINDOW // CHUNK
ROT_DIM = HEAD_DIM // 4
ROPE_THETA = 500000.0
D_FF = ((8 * D_MODEL // 3 + 255) // 256) * 256
FFN_CONV = 3
NORM_EPS = 1e-6

_IN_WIDTHS = (D_INNER, CONV_DIM, SSM_HEADS, N_HEADS * HEAD_DIM, N_KV_HEADS * HEAD_DIM, N_KV_HEADS * HEAD_DIM, 2 * D_MODEL)
IN_SPLITS = tuple(sum(_IN_WIDTHS[:i + 1]) for i in range(len(_IN_WIDTHS) - 1))
IN_WIDTH = sum(_IN_WIDTHS)

kernel_name = 'hybrid_ssd_swa_sink_convffn_stream_step'


def rmsnorm(x, g):
    xf = x.astype(jnp.float32)
    y = xf * lax.rsqrt(jnp.mean(xf * xf, axis=-1, keepdims=True) + NORM_EPS)
    return (y * g.astype(jnp.float32)).astype(x.dtype)


def partial_rope(x, pos):
    half = ROT_DIM // 2
    inv = ROPE_THETA ** (-jnp.arange(half, dtype=jnp.float32) / half)
    ang = pos.astype(jnp.float32)[:, None] * inv[None, :]
    cos, sin = jnp.cos(ang)[:, None, :], jnp.sin(ang)[:, None, :]
    xr = x[..., :ROT_DIM].astype(jnp.float32)
    x1, x2 = xr[..., :half], xr[..., half:]
    rot = jnp.concatenate([x1 * cos - x2 * sin, x2 * cos + x1 * sin], axis=-1)
    return jnp.concatenate([rot.astype(x.dtype), x[..., ROT_DIM:]], axis=-1)


def causal_dwconv(x, ctx, w, b):
    K, T = w.shape[0], x.shape[1]
    xp = jnp.concatenate([ctx.astype(x.dtype), x], axis=1)
    y = b
    for j in range(K):
        y = y + xp[:, j:j + T] * w[j]
    return y, xp[:, T:]


def ssd_chunk(h, inp):
    x, a, Bm, Cm = inp
    l = x.shape[1]
    acs = jnp.cumsum(a, axis=1)
    diff = acs[:, :, None] - acs[:, None, :]
    causal = jnp.tril(jnp.ones((l, l), dtype=bool))[None, :, :, None, None]
    decay = jnp.exp(jnp.where(causal, diff, -jnp.inf))
    cb = jnp.einsum('blgn,bsgn->blsg', Cm, Bm)
    y = jnp.einsum('blsgr,bsgrp->blgrp', cb[..., None] * decay, x)
    y = y + jnp.einsum('blgn,bgrpn->blgrp', Cm, h) * jnp.exp(acs)[..., None]
    xt = x * jnp.exp(acs[:, -1:] - acs)[..., None]
    h = h * jnp.exp(acs[:, -1])[..., None, None] + jnp.einsum('blgn,blgrp->bgrpn', Bm, xt)
    return h, y


def ssd_scan(x, a, Bm, Cm, h0):
    b, T = x.shape[0], x.shape[1]
    L = min(CHUNK, T)
    nc = T // L
    def blocks(t):
        return jnp.moveaxis(t.reshape((b, nc, L) + t.shape[2:]), 1, 0)
    h, y = lax.scan(ssd_chunk, h0, (blocks(x), blocks(a), blocks(Bm), blocks(Cm)))
    return jnp.moveaxis(y, 0, 1).reshape(x.shape), h


def ssm_branch(z, xbc, dt, conv_ctx, h0, p):
    b, T = z.shape[0], z.shape[1]
    xbc, conv_new = causal_dwconv(xbc, conv_ctx, p['ssm_conv_w'], p['ssm_conv_b'])
    xbc = jax.nn.silu(xbc).astype(jnp.float32)
    xs, Bm, Cm = jnp.split(xbc, [D_INNER, D_INNER + SSM_GROUPS * D_STATE], axis=-1)
    xs = xs.reshape(b, T, SSM_GROUPS, SSM_HPG, SSM_HEAD_DIM)
    Bm = Bm.reshape(b, T, SSM_GROUPS, D_STATE)
    Cm = Cm.reshape(b, T, SSM_GROUPS, D_STATE)
    dt = jax.nn.softplus(dt.astype(jnp.float32) + p['ssm_dt_bias'].astype(jnp.float32))
    dt = dt.reshape(b, T, SSM_GROUPS, SSM_HPG)
    A = -jnp.exp(p['ssm_a_log'].astype(jnp.float32)).reshape(SSM_GROUPS, SSM_HPG)
    h0 = h0.astype(jnp.float32).reshape(b, SSM_GROUPS, SSM_HPG, SSM_HEAD_DIM, D_STATE)
    y, h = ssd_scan(xs * dt[..., None], dt * A, Bm, Cm, h0)
    y = y + xs * p['ssm_d'].astype(jnp.float32).reshape(SSM_GROUPS, SSM_HPG)[..., None]
    y = y.reshape(b, T, D_INNER) * jax.nn.silu(z.astype(jnp.float32))
    yg = y.reshape(b, T, SSM_GROUPS, D_INNER // SSM_GROUPS)
    yg = yg * lax.rsqrt(jnp.mean(yg * yg, axis=-1, keepdims=True) + NORM_EPS)
    y = yg.reshape(b, T, D_INNER) * p['ssm_norm'].astype(jnp.float32)
    h = h.reshape(b, SSM_HEADS, SSM_HEAD_DIM, D_STATE)
    return y.astype(z.dtype), conv_new, h.astype(z.dtype)


def sink_softmax(s, sink):
    m = jnp.maximum(jnp.max(s, axis=-1, keepdims=True), sink)
    e = jnp.exp(s - m)
    return e / (jnp.sum(e, axis=-1, keepdims=True) + jnp.exp(sink - m))


def swa_prompt(q, k, v, sinks):
    b, T = q.shape[0], q.shape[1]
    nc = T // CHUNK
    qc = q.reshape(b, nc, CHUNK, N_KV_HEADS, Q_PER_KV, HEAD_DIM)
    def band(t):
        tc = t.reshape(b, nc, CHUNK, N_KV_HEADS, HEAD_DIM)
        tp = jnp.pad(tc, ((0, 0), (PREV_CHUNKS, 0), (0, 0), (0, 0), (0, 0)))
        return jnp.concatenate([tp[:, i:i + nc] for i in range(PREV_CHUNKS + 1)], axis=2)
    kb, vb = band(k), band(v)
    s = jnp.einsum('bcqkrd,bcskd->bckrqs', qc, kb, preferred_element_type=jnp.float32) * (HEAD_DIM ** -0.5)
    key_chunk = jnp.arange(nc)[:, None] + (jnp.arange((PREV_CHUNKS + 1) * CHUNK) // CHUNK - PREV_CHUNKS)[None, :]
    valid = (key_chunk >= 0)[None, :, None, None, None, :]
    s = jnp.where(valid, s, jnp.finfo(jnp.float32).min)
    pr = sink_softmax(s, sinks)
    o = jnp.einsum('bckrqs,bcskd->bcqkrd', pr.astype(v.dtype), vb)
    return o.reshape(b, T, N_HEADS * HEAD_DIM)


def swa_cached(q, k, v, cache_k, cache_v, sinks):
    b, S = q.shape[0], q.shape[1]
    kk = jnp.concatenate([cache_k.astype(k.dtype), k], axis=1)
    vv = jnp.concatenate([cache_v.astype(v.dtype), v], axis=1)
    qg = q.reshape(b, S, N_KV_HEADS, Q_PER_KV, HEAD_DIM)
    s = jnp.einsum('bqkrd,bskd->bkrqs', qg, kk, preferred_element_type=jnp.float32) * (HEAD_DIM ** -0.5)
    pr = sink_softmax(s, sinks)
    o = jnp.einsum('bkrqs,bskd->bqkrd', pr.astype(vv.dtype), vv).reshape(b, S, N_HEADS * HEAD_DIM)
    rows = cache_k.shape[1]
    return o, kk[:, -rows:], vv[:, -rows:]


def layer(x, c, pos, conv_ctx, h0, kv_ctx, ffn_ctx, p):
    b, T = x.shape[0], x.shape[1]
    mod = jax.nn.silu(c) @ p['w_ada'] + p['b_ada']
    sh1, sc1, g1, sh2, sc2, g2 = jnp.split(mod[:, None, :], 6, axis=-1)
    hn = rmsnorm(x, p['norm_mix_pre']) * (1 + sc1) + sh1
    proj = hn @ p['w_in']
    z, xbc, dt, q, k, v, gates = jnp.split(proj, IN_SPLITS, axis=-1)
    y_ssm, conv_new, h_new = ssm_branch(z, xbc, dt, conv_ctx, h0, p)
    q = partial_rope(q.reshape(b, T, N_HEADS, HEAD_DIM), pos)
    k = partial_rope(k.reshape(b, T, N_KV_HEADS, HEAD_DIM), pos)
    v = v.reshape(b, T, N_KV_HEADS, HEAD_DIM)
    sinks = p['attn_sinks'].astype(jnp.float32).reshape(N_KV_HEADS, Q_PER_KV, 1, 1)
    if kv_ctx is None:
        y_att = swa_prompt(q, k, v, sinks)
        rows = min(WINDOW, PAST_LEN)
        k_new, v_new = k[:, -rows:], v[:, -rows:]
    else:
        y_att, k_new, v_new = swa_cached(q, k, v, kv_ctx[0], kv_ctx[1], sinks)
    g_ssm, g_att = jnp.split(jax.nn.sigmoid(gates), 2, axis=-1)
    merged = g_ssm * (y_ssm @ p['w_ssm_out']) + g_att * (y_att @ p['w_att_out'])
    x = x + g1 * rmsnorm(merged @ p['w_out'], p['norm_mix_post'])
    hn = rmsnorm(x, p['norm_ffn_pre']) * (1 + sc2) + sh2
    gate, up = jnp.split(hn @ p['w_ffn_in'], 2, axis=-1)
    gate, ffn_new = causal_dwconv(gate, ffn_ctx, p['ffn_conv_w'], p['ffn_conv_b'])
    f = (jax.nn.silu(gate) * up) @ p['w_ffn_out']
    x = x + g2 * rmsnorm(f, p['norm_ffn_post'])
    return x, (h_new, conv_new, k_new, v_new, ffn_new)


def setup_inputs(seed: int = 0) -> dict:
    key = jax.random.key(seed)
    ks = jax.random.split(key, 40)
    f32 = jnp.float32
    def nrm(i, shape, scale):
        return jax.random.normal(ks[i], shape, f32) * scale
    def gain(i, shape):
        return 1.0 + 0.05 * jax.random.normal(ks[i], shape, f32)
    rows = min(WINDOW, PAST_LEN)
    dt0 = jnp.exp(jax.random.uniform(ks[30], (DEPTH, SSM_HEADS), f32, math.log(1e-3), math.log(1e-1)))
    return {
        'x_prompt': nrm(0, (BATCH, SEQ, D_MODEL), 1.0),
        'x_sample': nrm(1, (DEC_BATCH, DEC_SEQ, D_MODEL), 1.0),
        'cache_k': nrm(2, (DEPTH, DEC_BATCH, rows, N_KV_HEADS, HEAD_DIM), 1.0),
        'cache_v': nrm(3, (DEPTH, DEC_BATCH, rows, N_KV_HEADS, HEAD_DIM), 1.0),
        'state_ssm': nrm(4, (DEPTH, DEC_BATCH, SSM_HEADS, SSM_HEAD_DIM, D_STATE), 0.1),
        'state_conv': nrm(5, (DEPTH, DEC_BATCH, SSM_CONV - 1, CONV_DIM), 1.0),
        'state_ffn_conv': nrm(6, (DEPTH, DEC_BATCH, FFN_CONV - 1, D_FF), 1.0),
        'c_prompt': nrm(7, (BATCH, D_MODEL), 1.0),
        'c_sample': nrm(8, (DEC_BATCH, D_MODEL), 1.0),
        'w_ada': nrm(9, (DEPTH, D_MODEL, 6 * D_MODEL), 0.5 * D_MODEL ** -0.5),
        'b_ada': nrm(10, (DEPTH, 6 * D_MODEL), 0.02),
        'norm_mix_pre': gain(11, (DEPTH, D_MODEL)),
        'norm_mix_post': gain(12, (DEPTH, D_MODEL)),
        'w_in': nrm(13, (DEPTH, D_MODEL, IN_WIDTH), D_MODEL ** -0.5),
        'ssm_conv_w': nrm(14, (DEPTH, SSM_CONV, CONV_DIM), SSM_CONV ** -0.5),
        'ssm_conv_b': nrm(15, (DEPTH, CONV_DIM), 0.02),
        'ssm_dt_bias': dt0 + jnp.log(-jnp.expm1(-dt0)),
        'ssm_a_log': jnp.log(jax.random.uniform(ks[16], (DEPTH, SSM_HEADS), f32, 1.0, 16.0)),
        'ssm_d': gain(17, (DEPTH, SSM_HEADS)),
        'ssm_norm': gain(18, (DEPTH, D_INNER)),
        'attn_sinks': nrm(19, (DEPTH, N_HEADS), 0.5),
        'w_ssm_out': nrm(20, (DEPTH, D_INNER, D_MODEL), D_INNER ** -0.5),
        'w_att_out': nrm(21, (DEPTH, N_HEADS * HEAD_DIM, D_MODEL), (N_HEADS * HEAD_DIM) ** -0.5),
        'w_out': nrm(22, (DEPTH, D_MODEL, D_MODEL), D_MODEL ** -0.5),
        'norm_ffn_pre': gain(23, (DEPTH, D_MODEL)),
        'norm_ffn_post': gain(24, (DEPTH, D_MODEL)),
        'w_ffn_in': nrm(25, (DEPTH, D_MODEL, 2 * D_FF), D_MODEL ** -0.5),
        'ffn_conv_w': nrm(26, (DEPTH, FFN_CONV, D_FF), FFN_CONV ** -0.5),
        'ffn_conv_b': nrm(27, (DEPTH, D_FF), 0.02),
        'w_ffn_out': nrm(28, (DEPTH, D_FF, D_MODEL), D_FF ** -0.5),
    }


def reference(x_prompt, x_sample, cache_k, cache_v, state_ssm, state_conv, state_ffn_conv, c_prompt, c_sample,
              w_ada, b_ada, norm_mix_pre, norm_mix_post, w_in, ssm_conv_w, ssm_conv_b, ssm_dt_bias, ssm_a_log,
              ssm_d, ssm_norm, attn_sinks, w_ssm_out, w_att_out, w_out, norm_ffn_pre, norm_ffn_post,
              w_ffn_in, ffn_conv_w, ffn_conv_b, w_ffn_out):
    bp, tp = x_prompt.shape[0], x_prompt.shape[1]
    pos_p = jnp.arange(tp)
    pos_s = PAST_LEN + jnp.arange(x_sample.shape[1])
    yp, ys = x_prompt, x_sample
    st_p, st_s = [], []
    for l in range(DEPTH):
        p = {
            'w_ada': w_ada[l], 'b_ada': b_ada[l], 'norm_mix_pre': norm_mix_pre[l], 'norm_mix_post': norm_mix_post[l],
            'w_in': w_in[l], 'ssm_conv_w': ssm_conv_w[l], 'ssm_conv_b': ssm_conv_b[l], 'ssm_dt_bias': ssm_dt_bias[l],
            'ssm_a_log': ssm_a_log[l], 'ssm_d': ssm_d[l], 'ssm_norm': ssm_norm[l], 'attn_sinks': attn_sinks[l],
            'w_ssm_out': w_ssm_out[l], 'w_att_out': w_att_out[l], 'w_out': w_out[l],
            'norm_ffn_pre': norm_ffn_pre[l], 'norm_ffn_post': norm_ffn_post[l], 'w_ffn_in': w_ffn_in[l],
            'ffn_conv_w': ffn_conv_w[l], 'ffn_conv_b': ffn_conv_b[l], 'w_ffn_out': w_ffn_out[l],
        }
        conv0 = jnp.zeros((bp, SSM_CONV - 1, CONV_DIM), x_prompt.dtype)
        h0 = jnp.zeros((bp, SSM_HEADS, SSM_HEAD_DIM, D_STATE), jnp.float32)
        ffn0 = jnp.zeros((bp, FFN_CONV - 1, D_FF), x_prompt.dtype)
        yp, sp = layer(yp, c_prompt, pos_p, conv0, h0, None, ffn0, p)
        ys, ss = layer(ys, c_sample, pos_s, state_conv[l], state_ssm[l], (cache_k[l], cache_v[l]), state_ffn_conv[l], p)
        st_p.append(sp)
        st_s.append(ss)
    ssm_p, conv_p, k_p, v_p, ffn_p = [jnp.stack(t) for t in zip(*st_p)]
    ssm_s, conv_s, k_s, v_s, ffn_s = [jnp.stack(t) for t in zip(*st_s)]
    return (yp, ys, ssm_p, conv_p, k_p, v_p, ffn_p, ssm_s, conv_s, k_s, v_s, ffn_s)
```

```python
import functools
import math

import jax
import jax.numpy as jnp
from jax import lax
from jax.experimental import pallas as pl
from jax.experimental.pallas import tpu as pltpu

F32 = jnp.float32
BF16 = jnp.bfloat16

D_MODEL = 1024
CHUNK = 64
D_INNER = 2048
SSM_HEADS = 32
SSM_GROUPS = 8
SSM_HPG = 4
SSM_HEAD_DIM = 64
GROUP_W = SSM_HPG * SSM_HEAD_DIM
D_STATE = 128
SSM_CONV = 4
CONV_DIM = 4096
HEAD_DIM = 64
N_HEADS = 16
N_KV_HEADS = 4
Q_PER_KV = 4
KV_W = N_KV_HEADS * HEAD_DIM
WINDOW = 128
ROT_DIM = 16
ROPE_THETA = 500000.0
D_FF = 2816
FFN_CONV = 3
PAST_LEN = 2048
NORM_EPS = 1e-6
LANES = 128
SUBLANES = 8
VMEM_LIMIT = 58 * 1024 * 1024

_C_Z = 0
_C_XBC = _C_Z + D_INNER
_C_Q = _C_XBC + CONV_DIM
_C_K = _C_Q + N_HEADS * HEAD_DIM
_C_V = _C_K + KV_W
_C_G = _C_V + KV_W
_C_END = _C_G + 2 * D_MODEL

_NT = (((1,), (1,)), ((), ()))


def _sigmoid(x):
    return 1.0 / (1.0 + jnp.exp(-x))


def _silu(x):
    return x * _sigmoid(x)


def _rms(x, g):
    return x * lax.rsqrt(jnp.mean(x * x, axis=-1, keepdims=True) + NORM_EPS) * g


def _resident(shape):
    nd = len(shape)
    return pl.BlockSpec(shape, lambda *_: (0,) * nd, pipeline_mode=pl.Buffered(1))


def _ada_body(c_ref, w_ref, b_ref, o_ref):
    s = _silu(c_ref[...]).astype(BF16)
    o_ref[...] = jnp.dot(s, w_ref[...].astype(BF16), preferred_element_type=F32) + b_ref[...]


def _ada(c, w_ada, b_ada):
    bc, n = c.shape[0], w_ada.shape[1]
    tn = 1024
    return pl.pallas_call(
        _ada_body,
        grid=(n // tn,),
        in_specs=[pl.BlockSpec((bc, D_MODEL), lambda j: (0, 0)),
                  pl.BlockSpec((D_MODEL, tn), lambda j: (0, j)),
                  pl.BlockSpec((1, tn), lambda j: (0, j))],
        out_specs=pl.BlockSpec((bc, tn), lambda j: (0, j)),
        out_shape=jax.ShapeDtypeStruct((bc, n), F32),
        name="ada",
    )(c, w_ada, b_ada.reshape(1, n))


def _rope(xf, cos, s_lo, s_hi):
    outs = []
    for j in range(xf.shape[1] // LANES):
        xb = xf[:, LANES * j:LANES * (j + 1)]
        outs.append(xb * cos + pltpu.roll(xb, LANES - ROT_DIM // 2, 1) * s_lo
                    + pltpu.roll(xb, ROT_DIM // 2, 1) * s_hi)
    return outs[0] if len(outs) == 1 else jnp.concatenate(outs, axis=1)


def _inproj_body(x_ref, sc_ref, sh_ref, g_ref, w_ref, wdt_ref, cos_ref, slo_ref, shi_ref,
                 z_ref, xbc_ref, dt_ref, q_ref, k_ref, v_ref, gt_ref, klast_ref, vlast_ref, clast_ref,
                 *, tm, keep):
    x = x_ref[...]
    hn = _rms(x, g_ref[...]) * (1.0 + sc_ref[...]) + sh_ref[...]
    hb = hn.astype(BF16)
    cw = 1024

    def mm(c0, c1):
        return jnp.dot(hb, w_ref[:, c0:c1], preferred_element_type=F32)

    for c in range(0, D_INNER, cw):
        z_ref[:, c:c + cw] = mm(_C_Z + c, _C_Z + c + cw).astype(BF16)
    for c in range(0, CONV_DIM, cw):
        r = mm(_C_XBC + c, _C_XBC + c + cw)
        xbc_ref[:, c:c + cw] = r.astype(BF16)
        clast_ref[:, c:c + cw] = r[tm - SUBLANES:tm, :]
    cos, slo, shi = cos_ref[...], slo_ref[...], shi_ref[...]
    qf = _rope(mm(_C_Q, _C_K), cos, slo, shi)
    q_ref[...] = (qf * (HEAD_DIM ** -0.5)).astype(BF16)
    kf = _rope(mm(_C_K, _C_V), cos, slo, shi)
    k_ref[...] = kf.astype(BF16)
    klast_ref[...] = kf[tm - keep:tm, :]
    vf = mm(_C_V, _C_G)
    v_ref[...] = vf.astype(BF16)
    vlast_ref[...] = vf[tm - keep:tm, :]
    for c in range(0, 2 * D_MODEL, cw):
        gt_ref[:, c:c + cw] = mm(_C_G + c, _C_G + c + cw).astype(BF16)
    dt_ref[...] = jnp.dot(hb, wdt_ref[...], preferred_element_type=F32)


def _inproj(x, sc, sh, g, w_main, w_dt, cos, slo, shi, tm):
    b, t, _ = x.shape
    keep = min(WINDOW, t)
    assert t % tm == 0 and tm >= keep and tm % SUBLANES == 0
    row = lambda w: pl.BlockSpec((None, tm, w), lambda bi, i: (bi, i, 0))
    per_b = lambda r, w: pl.BlockSpec((None, r, w), lambda bi, i: (bi, 0, 0))
    tab = pl.BlockSpec((tm, LANES), lambda bi, i: (i, 0))
    sds = lambda w, dt: jax.ShapeDtypeStruct((b, t, w), dt)
    return pl.pallas_call(
        functools.partial(_inproj_body, tm=tm, keep=keep),
        grid=(b, t // tm),
        in_specs=[row(D_MODEL), per_b(1, D_MODEL), per_b(1, D_MODEL), _resident((1, D_MODEL)),
                  _resident(w_main.shape), _resident(w_dt.shape), tab, tab, tab],
        out_specs=[row(D_INNER), row(CONV_DIM), row(LANES), row(N_HEADS * HEAD_DIM), row(KV_W), row(KV_W),
                   row(2 * D_MODEL), per_b(keep, KV_W), per_b(keep, KV_W), per_b(SUBLANES, CONV_DIM)],
        out_shape=[sds(D_INNER, BF16), sds(CONV_DIM, BF16), sds(LANES, F32), sds(N_HEADS * HEAD_DIM, BF16),
                   sds(KV_W, BF16), sds(KV_W, BF16), sds(2 * D_MODEL, BF16),
                   jax.ShapeDtypeStruct((b, keep, KV_W), F32), jax.ShapeDtypeStruct((b, keep, KV_W), F32),
                   jax.ShapeDtypeStruct((b, SUBLANES, CONV_DIM), F32)],
        compiler_params=pltpu.CompilerParams(dimension_semantics=("arbitrary", "arbitrary"),
                                             vmem_limit_bytes=VMEM_LIMIT),
        name="inproj",
    )(x, sc, sh, g, w_main, w_dt, cos, slo, shi)


def _split3(v):
    hi = v.astype(BF16)
    r1 = v - hi.astype(F32)
    mid = r1.astype(BF16)
    lo = (r1 - mid.astype(F32)).astype(BF16)
    return hi, mid, lo


def _exact_dot(m01, v, dims=None):
    out = None
    for p in _split3(v):
        if dims is None:
            d = jnp.dot(m01, p, preferred_element_type=F32)
        else:
            d = lax.dot_general(m01, p, dims, preferred_element_type=F32)
        out = d if out is None else out + d
    return out


def _expand(v, g, lane):
    n = v.shape[0]
    b = [jnp.broadcast_to(v[:, SSM_HPG * g + r:SSM_HPG * g + r + 1], (n, GROUP_W)) for r in range(SSM_HPG)]
    return jnp.where(lane < 64, b[0], jnp.where(lane < 128, b[1], jnp.where(lane < 192, b[2], b[3])))


def _ssd_body(z_ref, xbc_ref, dt_ref, cw_ref, cb_ref, dtb_ref, alog_ref, dexp_ref, nw_ref, h0_ref, c0_ref,
              y_ref, hout_ref, h_s, xe_s, xc_s, dt_s, a_s, *, ts, L):
    t = pl.program_id(1)

    @pl.when(t == 0)
    def _():
        h_s[...] = h0_ref[...]
        xe_s[0:SUBLANES, :] = c0_ref[...]

    xe_s[SUBLANES:SUBLANES + ts, :] = xbc_ref[...].astype(F32)
    cwid = 512
    for c in range(0, CONV_DIM, cwid):
        cs = slice(c, c + cwid)
        acc = cb_ref[:, cs]
        for j in range(SSM_CONV):
            off = SUBLANES - (SSM_CONV - 1) + j
            acc = acc + xe_s[off:off + ts, cs] * cw_ref[j:j + 1, cs]
        xc_s[:, cs] = _silu(acc)
    xe_s[0:SUBLANES, :] = xe_s[ts:ts + SUBLANES, :]

    dtr = dt_ref[...] + dtb_ref[...]
    dtv = jnp.maximum(dtr, 0.0) + jnp.log1p(jnp.exp(-jnp.abs(dtr)))
    dt_s[...] = dtv
    a_s[...] = dtv * (-jnp.exp(alog_ref[...]))

    ri = lax.broadcasted_iota(jnp.int32, (L, L), 0)
    ci = lax.broadcasted_iota(jnp.int32, (L, L), 1)
    causal = ci <= ri
    tril = causal.astype(F32).astype(BF16)
    e_r = lax.broadcasted_iota(jnp.int32, (LANES, LANES), 0)
    e_c = lax.broadcasted_iota(jnp.int32, (LANES, LANES), 1)
    eye = (e_r == e_c).astype(F32).astype(BF16)
    lane = lax.broadcasted_iota(jnp.int32, (L, GROUP_W), 1)
    lane1 = lax.broadcasted_iota(jnp.int32, (1, GROUP_W), 1)

    def chunk(ck, carry):
        rows = pl.ds(pl.multiple_of(ck * L, L), L)
        a_c = a_s[rows, :]
        dt_c = dt_s[rows, :]
        acs = _exact_dot(tril, a_c)
        acs_last = acs[L - 1:L, :]
        eacs = jnp.exp(acs)
        wfull = dt_c * jnp.exp(acs_last - acs)
        declast = jnp.exp(acs_last)
        acs_t = _exact_dot(eye, acs, _NT)
        dt_t = _exact_dot(eye, dt_c, _NT)
        for g in range(SSM_GROUPS):
            cols = slice(GROUP_W * g, GROUP_W * (g + 1))
            xs_g = xc_s[rows, cols]
            bb = xc_s[rows, D_INNER + D_STATE * g:D_INNER + D_STATE * (g + 1)].astype(BF16)
            cbf = xc_s[rows, D_INNER + SSM_GROUPS * D_STATE + D_STATE * g:
                       D_INNER + SSM_GROUPS * D_STATE + D_STATE * (g + 1)].astype(BF16)
            cbm = lax.dot_general(cbf, bb, _NT, preferred_element_type=F32)
            ms = []
            for r in range(SSM_HPG):
                h = SSM_HPG * g + r
                diff = acs[:, h:h + 1] - acs_t[h:h + 1, :]
                dec = jnp.exp(jnp.where(causal, diff, -jnp.inf))
                ms.append((cbm * dec * dt_t[h:h + 1, :]).astype(BF16))
            rm = jnp.dot(jnp.concatenate(ms, axis=0), xs_g.astype(BF16), preferred_element_type=F32)
            y_in = jnp.where(lane < 64, rm[0:L],
                             jnp.where(lane < 128, rm[L:2 * L], jnp.where(lane < 192, rm[2 * L:3 * L], rm[3 * L:])))
            hg = h_s[g]
            y_x = jnp.dot(cbf, hg.astype(BF16), preferred_element_type=F32) * _expand(eacs, g, lane)
            xw = (xs_g * _expand(wfull, g, lane)).astype(BF16)
            b_t = lax.dot_general(eye, bb, _NT, preferred_element_type=F32).astype(BF16)
            h_s[g] = hg * _expand(declast, g, lane1) + jnp.dot(b_t, xw, preferred_element_type=F32)
            yg = y_in + y_x + xs_g * dexp_ref[:, cols]
            yg = yg * _silu(z_ref[rows, cols].astype(F32))
            y_ref[rows, cols] = _rms(yg, nw_ref[:, cols]).astype(BF16)
        return carry

    lax.fori_loop(0, ts // L, chunk, 0)

    @pl.when(t == pl.num_programs(1) - 1)
    def _():
        hout_ref[...] = h_s[...]


def _ssd(z, xbc, dt, conv_w, conv_b, dtb, alog, dexp, norm_w, h0, c0, ts, L):
    b, t, _ = z.shape
    assert t % ts == 0 and ts % L == 0
    row = lambda w: pl.BlockSpec((None, ts, w), lambda bi, i: (bi, i, 0))
    st = pl.BlockSpec((None, SSM_GROUPS, D_STATE, GROUP_W), lambda bi, i: (bi, 0, 0, 0))
    return pl.pallas_call(
        functools.partial(_ssd_body, ts=ts, L=L),
        grid=(b, t // ts),
        in_specs=[row(D_INNER), row(CONV_DIM), row(LANES), _resident(conv_w.shape), _resident(conv_b.shape),
                  _resident(dtb.shape), _resident(alog.shape), _resident(dexp.shape), _resident(norm_w.shape),
                  st, pl.BlockSpec((None, SUBLANES, CONV_DIM), lambda bi, i: (bi, 0, 0))],
        out_specs=[row(D_INNER), st],
        out_shape=[jax.ShapeDtypeStruct((b, t, D_INNER), BF16),
                   jax.ShapeDtypeStruct((b, SSM_GROUPS, D_STATE, GROUP_W), F32)],
        scratch_shapes=[pltpu.VMEM((SSM_GROUPS, D_STATE, GROUP_W), F32),
                        pltpu.VMEM((SUBLANES + ts, CONV_DIM), F32),
                        pltpu.VMEM((ts, CONV_DIM), F32),
                        pltpu.VMEM((ts, LANES), F32),
                        pltpu.VMEM((ts, LANES), F32)],
        compiler_params=pltpu.CompilerParams(dimension_semantics=("arbitrary", "arbitrary"),
                                             vmem_limit_bytes=VMEM_LIMIT),
        name="ssd",
    )(z, xbc, dt, conv_w, conv_b, dtb, alog, dexp, norm_w, h0, c0)


def _attn_body(sink_ref, q_ref, kc_ref, kp_ref, vc_ref, vp_ref, o_ref, kk_s, vv_s, *, tq, lq, has_cache):
    i = pl.program_id(1)
    kk_s[0:WINDOW, :] = kp_ref[...]
    kk_s[WINDOW:WINDOW + tq, :] = kc_ref[...]
    vv_s[0:WINDOW, :] = vp_ref[...]
    vv_s[WINDOW:WINDOW + tq, :] = vc_ref[...]
    w = WINDOW + lq
    kpos = lax.broadcasted_iota(jnp.int32, (1, w), 1) + (i * tq - WINDOW)
    for j in range(tq // lq):
        krows = slice(lq * j, lq * j + w)
        qrows = slice(lq * j, lq * (j + 1))
        valid = (kpos + lq * j) >= 0
        outs = []
        for kh in range(N_KV_HEADS):
            kc = kk_s[krows, HEAD_DIM * kh:HEAD_DIM * (kh + 1)]
            vc = vv_s[krows, HEAD_DIM * kh:HEAD_DIM * (kh + 1)]
            for r in range(Q_PER_KV):
                h = Q_PER_KV * kh + r
                qh = q_ref[qrows, HEAD_DIM * h:HEAD_DIM * (h + 1)]
                s = lax.dot_general(qh, kc, _NT, preferred_element_type=F32)
                if not has_cache:
                    s = jnp.where(valid, s, jnp.finfo(F32).min)
                sk = sink_ref[h]
                m = jnp.maximum(jnp.max(s, axis=-1, keepdims=True), sk)
                e = jnp.exp(s - m)
                den = jnp.sum(e, axis=-1, keepdims=True) + jnp.exp(sk - m)
                outs.append(jnp.dot((e / den).astype(BF16), vc, preferred_element_type=F32))
        o_ref[qrows, :] = jnp.concatenate(outs, axis=1).astype(BF16)


def _attn(sinks, q, k, v, k_prev, v_prev, tq, lq, has_cache):
    b, t, _ = q.shape
    assert t % tq == 0 and tq % lq == 0
    nb = tq // WINDOW if not has_cache else 0
    row = lambda w: pl.BlockSpec((None, tq, w), lambda bi, i: (bi, i, 0))
    if has_cache:
        prev = pl.BlockSpec((None, WINDOW, KV_W), lambda bi, i: (bi, 0, 0))
    else:
        assert tq % WINDOW == 0
        prev = pl.BlockSpec((None, WINDOW, KV_W), lambda bi, i: (bi, jnp.maximum(i * nb - 1, 0), 0))
    return pl.pallas_call(
        functools.partial(_attn_body, tq=tq, lq=lq, has_cache=has_cache),
        grid=(b, t // tq),
        in_specs=[pl.BlockSpec(memory_space=pltpu.SMEM), row(N_HEADS * HEAD_DIM), row(KV_W), prev, row(KV_W), prev],
        out_specs=row(N_HEADS * HEAD_DIM),
        out_shape=jax.ShapeDtypeStruct((b, t, N_HEADS * HEAD_DIM), BF16),
        scratch_shapes=[pltpu.VMEM((WINDOW + tq, KV_W), BF16), pltpu.VMEM((WINDOW + tq, KV_W), BF16)],
        compiler_params=pltpu.CompilerParams(dimension_semantics=("arbitrary", "arbitrary")),
        name="attn",
    )(sinks, q, k, k_prev, v, v_prev)


def _post_body(x_ref, ys_ref, ya_ref, gt_ref, g1_ref, sh2_ref, sc2_ref, g2_ref, npost_ref, nfpre_ref, nfpost_ref,
               wso_ref, wao_ref, wo_ref, wfi_ref, fcw_ref, fcb_ref, wfo_ref, f0_ref,
               y_ref, flast_ref, ge_s, *, tm):
    t = pl.program_id(1)

    @pl.when(t == 0)
    def _():
        ge_s[0:SUBLANES, :] = f0_ref[...]

    a = jnp.dot(ys_ref[...], wso_ref[...], preferred_element_type=F32)
    b = jnp.dot(ya_ref[...], wao_ref[...], preferred_element_type=F32)
    merged = (_sigmoid(gt_ref[:, 0:D_MODEL].astype(F32)) * a
              + _sigmoid(gt_ref[:, D_MODEL:2 * D_MODEL].astype(F32)) * b)
    o = jnp.dot(merged.astype(BF16), wo_ref[...], preferred_element_type=F32)
    x1 = x_ref[...] + g1_ref[...] * _rms(o, npost_ref[...])
    hn = (_rms(x1, nfpre_ref[...]) * (1.0 + sc2_ref[...]) + sh2_ref[...]).astype(BF16)
    ge_s[SUBLANES:SUBLANES + tm, :] = jnp.dot(hn, wfi_ref[:, 0:D_FF], preferred_element_type=F32)
    up = jnp.dot(hn, wfi_ref[:, D_FF:2 * D_FF], preferred_element_type=F32)
    gc = fcb_ref[...]
    for j in range(FFN_CONV):
        off = SUBLANES - (FFN_CONV - 1) + j
        gc = gc + ge_s[off:off + tm, :] * fcw_ref[j:j + 1, :]
    tail = ge_s[tm:tm + SUBLANES, :]
    flast_ref[...] = tail
    ge_s[0:SUBLANES, :] = tail
    f = jnp.dot((_silu(gc) * up).astype(BF16), wfo_ref[...], preferred_element_type=F32)
    y_ref[...] = x1 + g2_ref[...] * _rms(f, nfpost_ref[...])


def _post(x, ys, ya, gt, g1, sh2, sc2, g2, npost, nfpre, nfpost, wso, wao, wo, wfi, fcw, fcb, wfo, f0, tm):
    b, t, _ = x.shape
    assert t % tm == 0 and tm % SUBLANES == 0
    row = lambda w: pl.BlockSpec((None, tm, w), lambda bi, i: (bi, i, 0))
    per_b = lambda r, w: pl.BlockSpec((None, r, w), lambda bi, i: (bi, 0, 0))
    vec = _resident((1, D_MODEL))
    return pl.pallas_call(
        functools.partial(_post_body, tm=tm),
        grid=(b, t // tm),
        in_specs=[row(D_MODEL), row(D_INNER), row(N_HEADS * HEAD_DIM), row(2 * D_MODEL),
                  per_b(1, D_MODEL), per_b(1, D_MODEL), per_b(1, D_MODEL), per_b(1, D_MODEL), vec, vec, vec,
                  _resident(wso.shape), _resident(wao.shape), _resident(wo.shape), _resident(wfi.shape),
                  _resident(fcw.shape), _resident(fcb.shape), _resident(wfo.shape), per_b(SUBLANES, D_FF)],
        out_specs=[row(D_MODEL), per_b(SUBLANES, D_FF)],
        out_shape=[jax.ShapeDtypeStruct((b, t, D_MODEL), F32), jax.ShapeDtypeStruct((b, SUBLANES, D_FF), F32)],
        scratch_shapes=[pltpu.VMEM((SUBLANES + tm, D_FF), F32)],
        compiler_params=pltpu.CompilerParams(dimension_semantics=("arbitrary", "arbitrary"),
                                             vmem_limit_bytes=VMEM_LIMIT),
        name="post",
    )(x, ys, ya, gt, g1, sh2, sc2, g2, npost, nfpre, nfpost, wso, wao, wo, wfi, fcw, fcb, wfo, f0)


def _rope_tables(pos):
    half = ROT_DIM // 2
    inv = ROPE_THETA ** (-jnp.arange(half, dtype=F32) / half)
    ang = pos.astype(F32)[:, None] * inv[None, :]
    cos, sin = jnp.cos(ang), jnp.sin(ang)
    t = pos.shape[0]
    pad = jnp.zeros((t, HEAD_DIM - ROT_DIM), F32)
    zero = jnp.zeros((t, half), F32)
    c64 = jnp.concatenate([cos, cos, pad + 1.0], axis=1)
    lo64 = jnp.concatenate([-sin, zero, pad], axis=1)
    hi64 = jnp.concatenate([zero, sin, pad], axis=1)
    two = lambda a: jnp.concatenate([a, a], axis=1)
    return two(c64), two(lo64), two(hi64)


def _pad_rows(a, rows):
    b, r, w = a.shape
    return jnp.concatenate([jnp.zeros((b, rows - r, w), a.dtype), a], axis=1)


def _prep_weights(p):
    w_in = p['w_in']
    c_dt0 = D_INNER + CONV_DIM
    w_main = jnp.concatenate([w_in[:, :c_dt0], w_in[:, c_dt0 + SSM_HEADS:]], axis=1).astype(BF16)
    w_dt = jnp.pad(w_in[:, c_dt0:c_dt0 + SSM_HEADS], ((0, 0), (0, LANES - SSM_HEADS))).astype(BF16)
    padh = lambda a: jnp.pad(a.astype(F32), (0, LANES - SSM_HEADS)).reshape(1, LANES)
    return dict(
        w_main=w_main, w_dt=w_dt,
        g_pre=p['norm_mix_pre'].reshape(1, D_MODEL),
        conv_w=p['ssm_conv_w'], conv_b=p['ssm_conv_b'].reshape(1, CONV_DIM),
        dtb=padh(p['ssm_dt_bias']), alog=padh(p['ssm_a_log']),
        dexp=jnp.repeat(p['ssm_d'].astype(F32), SSM_HEAD_DIM).reshape(1, D_INNER),
        ssm_norm=p['ssm_norm'].reshape(1, D_INNER),
        sinks=p['attn_sinks'].astype(F32),
        wso=p['w_ssm_out'].astype(BF16), wao=p['w_att_out'].astype(BF16), wo=p['w_out'].astype(BF16),
        npost=p['norm_mix_post'].reshape(1, D_MODEL), nfpre=p['norm_ffn_pre'].reshape(1, D_MODEL),
        nfpost=p['norm_ffn_post'].reshape(1, D_MODEL),
        wfi=p['w_ffn_in'].astype(BF16), fcw=p['ffn_conv_w'], fcb=p['ffn_conv_b'].reshape(1, D_FF),
        wfo=p['w_ffn_out'].astype(BF16),
    )


def _layer(x, mod, pos, conv_ctx, h0, kv_ctx, ffn_ctx, w, tiles):
    b, t, _ = x.shape
    tm_in, ts, L, tq, lq, tm_post = tiles
    sh1, sc1, g1, sh2, sc2, g2 = [m.reshape(b, 1, D_MODEL) for m in jnp.split(mod, 6, axis=-1)]
    cos, slo, shi = _rope_tables(pos)
    z, xbc, dt, q, k, v, gt, k_last, v_last, c_last = _inproj(
        x, sc1, sh1, w['g_pre'], w['w_main'], w['w_dt'], cos, slo, shi, tm_in)

    h0_t = h0.astype(F32).reshape(b, SSM_GROUPS, SSM_HPG, SSM_HEAD_DIM, D_STATE)
    h0_t = jnp.transpose(h0_t, (0, 1, 4, 2, 3)).reshape(b, SSM_GROUPS, D_STATE, GROUP_W)
    y_ssm, h_t = _ssd(z, xbc, dt, w['conv_w'], w['conv_b'], w['dtb'], w['alog'], w['dexp'], w['ssm_norm'],
                      h0_t, _pad_rows(conv_ctx.astype(F32), SUBLANES), ts, L)
    h_new = jnp.transpose(h_t.reshape(b, SSM_GROUPS, D_STATE, SSM_HPG, SSM_HEAD_DIM), (0, 1, 3, 4, 2))
    h_new = h_new.reshape(b, SSM_HEADS, SSM_HEAD_DIM, D_STATE)
    conv_new = c_last[:, SUBLANES - (SSM_CONV - 1):, :]

    keep = k_last.shape[1]
    if kv_ctx is None:
        y_att = _attn(w['sinks'], q, k, v, k, v, tq, lq, False)
        k_new = k_last.reshape(b, keep, N_KV_HEADS, HEAD_DIM)
        v_new = v_last.reshape(b, keep, N_KV_HEADS, HEAD_DIM)
    else:
        ck, cv = kv_ctx
        rows = ck.shape[1]
        y_att = _attn(w['sinks'], q, k, v, ck.reshape(b, rows, KV_W).astype(BF16),
                      cv.reshape(b, rows, KV_W).astype(BF16), tq, lq, True)
        k_new = jnp.concatenate([ck, k_last.reshape(b, keep, N_KV_HEADS, HEAD_DIM)], axis=1)[:, -rows:]
        v_new = jnp.concatenate([cv, v_last.reshape(b, keep, N_KV_HEADS, HEAD_DIM)], axis=1)[:, -rows:]

    y, f_last = _post(x, y_ssm, y_att, gt, g1, sh2, sc2, g2, w['npost'], w['nfpre'], w['nfpost'],
                      w['wso'], w['wao'], w['wo'], w['wfi'], w['fcw'], w['fcb'], w['wfo'],
                      _pad_rows(ffn_ctx.astype(F32), SUBLANES), tm_post)
    ffn_new = f_last[:, SUBLANES - (FFN_CONV - 1):, :]
    return y, (h_new, conv_new, k_new, v_new, ffn_new)


def _tiles(t):
    L = min(CHUNK, t)
    return (min(256, t), min(256, t), L, min(256, t), L, min(256, t))


def kernel(x_prompt, x_sample, cache_k, cache_v, state_ssm, state_conv, state_ffn_conv, c_prompt, c_sample,
           w_ada, b_ada, norm_mix_pre, norm_mix_post, w_in, ssm_conv_w, ssm_conv_b, ssm_dt_bias, ssm_a_log,
           ssm_d, ssm_norm, attn_sinks, w_ssm_out, w_att_out, w_out, norm_ffn_pre, norm_ffn_post,
           w_ffn_in, ffn_conv_w, ffn_conv_b, w_ffn_out):
    depth = w_in.shape[0]
    bp, tp, _ = x_prompt.shape
    bs, tsamp, _ = x_sample.shape
    pos_p = jnp.arange(tp)
    pos_s = PAST_LEN + jnp.arange(tsamp)
    yp, ys = x_prompt, x_sample
    st_p, st_s = [], []
    for l in range(depth):
        p = {
            'w_ada': w_ada[l], 'b_ada': b_ada[l], 'norm_mix_pre': norm_mix_pre[l], 'norm_mix_post': norm_mix_post[l],
            'w_in': w_in[l], 'ssm_conv_w': ssm_conv_w[l], 'ssm_conv_b': ssm_conv_b[l], 'ssm_dt_bias': ssm_dt_bias[l],
            'ssm_a_log': ssm_a_log[l], 'ssm_d': ssm_d[l], 'ssm_norm': ssm_norm[l], 'attn_sinks': attn_sinks[l],
            'w_ssm_out': w_ssm_out[l], 'w_att_out': w_att_out[l], 'w_out': w_out[l],
            'norm_ffn_pre': norm_ffn_pre[l], 'norm_ffn_post': norm_ffn_post[l], 'w_ffn_in': w_ffn_in[l],
            'ffn_conv_w': ffn_conv_w[l], 'ffn_conv_b': ffn_conv_b[l], 'w_ffn_out': w_ffn_out[l],
        }
        w = _prep_weights(p)
        mod = _ada(jnp.concatenate([c_prompt, c_sample], axis=0), p['w_ada'], p['b_ada'])
        conv0 = jnp.zeros((bp, SSM_CONV - 1, CONV_DIM), F32)
        h0 = jnp.zeros((bp, SSM_HEADS, SSM_HEAD_DIM, D_STATE), F32)
        ffn0 = jnp.zeros((bp, FFN_CONV - 1, D_FF), F32)
        yp, sp = _layer(yp, mod[:bp], pos_p, conv0, h0, None, ffn0, w, _tiles(tp))
        ys, ss = _layer(ys, mod[bp:], pos_s, state_conv[l], state_ssm[l], (cache_k[l], cache_v[l]),
                        state_ffn_conv[l], w, _tiles(tsamp))
        st_p.append(sp)
        st_s.append(ss)
    ssm_p, conv_p, k_p, v_p, ffn_p = [jnp.stack(t) for t in zip(*st_p)]
    ssm_s, conv_s, k_s, v_s, ffn_s = [jnp.stack(t) for t in zip(*st_s)]
    return (yp, ys, ssm_p, conv_p, k_p, v_p, ffn_p, ssm_s, conv_s, k_s, v_s, ffn_s)
```

```python
import functools

import jax
import jax.numpy as jnp
from jax import lax
from jax.experimental import pallas as pl
from jax.experimental.pallas import tpu as pltpu

F32 = jnp.float32
BF16 = jnp.bfloat16

D_MODEL = 1024
CHUNK = 64
D_INNER = 2048
SSM_HEADS = 32
SSM_GROUPS = 8
SSM_HPG = 4
SSM_HEAD_DIM = 64
GROUP_W = SSM_HPG * SSM_HEAD_DIM
D_STATE = 128
SSM_CONV = 4
CONV_DIM = 4096
HEAD_DIM = 64
N_HEADS = 16
N_KV_HEADS = 4
Q_PER_KV = 4
Q_W = N_HEADS * HEAD_DIM
KV_W = N_KV_HEADS * HEAD_DIM
KVD_W = 2 * KV_W
WINDOW = 128
ROT_DIM = 16
ROPE_THETA = 500000.0
D_FF = 2816
FFN_CONV = 3
PAST_LEN = 2048
NORM_EPS = 1e-6
LANES = 128
SUBLANES = 8
VMEM_LIMIT = 58 * 1024 * 1024

_C_Z = 0
_C_XBC = _C_Z + D_INNER
_C_Q = _C_XBC + CONV_DIM
_C_K = _C_Q + Q_W
_C_V = _C_K + KVD_W
_C_G = _C_V + KVD_W
_C_END = _C_G + 2 * D_MODEL

_NT = (((1,), (1,)), ((), ()))


def _sigmoid(x):
    return 1.0 / (1.0 + jnp.exp(-x))


def _silu(x):
    return x * _sigmoid(x)


def _rms(x, g):
    return x * lax.rsqrt(jnp.mean(x * x, axis=-1, keepdims=True) + NORM_EPS) * g


def _resident(shape):
    nd = len(shape)
    return pl.BlockSpec(shape, lambda *_: (0,) * nd, pipeline_mode=pl.Buffered(1))


def _params(vmem=None):
    return pltpu.CompilerParams(dimension_semantics=("arbitrary", "arbitrary"), vmem_limit_bytes=vmem)


def _ada_body(c_ref, w_ref, b_ref, o_ref):
    s = _silu(c_ref[...]).astype(BF16)
    o_ref[...] = jnp.dot(s, w_ref[...].astype(BF16), preferred_element_type=F32) + b_ref[...]


def _ada(c, w_ada, b_ada):
    bc, n = c.shape[0], w_ada.shape[1]
    tn = 1024
    return pl.pallas_call(
        _ada_body,
        grid=(n // tn,),
        in_specs=[pl.BlockSpec((bc, D_MODEL), lambda j: (0, 0)),
                  pl.BlockSpec((D_MODEL, tn), lambda j: (0, j)),
                  pl.BlockSpec((1, tn), lambda j: (0, j))],
        out_specs=pl.BlockSpec((bc, tn), lambda j: (0, j)),
        out_shape=jax.ShapeDtypeStruct((bc, n), F32),
        name="ada",
    )(c, w_ada, b_ada.reshape(1, n))


def _rope(xf, cos, s_lo, s_hi):
    outs = []
    for j in range(xf.shape[1] // LANES):
        xb = xf[:, LANES * j:LANES * (j + 1)]
        outs.append(xb * cos + pltpu.roll(xb, LANES - ROT_DIM // 2, 1) * s_lo
                    + pltpu.roll(xb, ROT_DIM // 2, 1) * s_hi)
    return outs[0] if len(outs) == 1 else jnp.concatenate(outs, axis=1)


def _inproj_body(x_ref, sc_ref, sh_ref, g_ref, w_ref, wdt_ref, cos_ref, slo_ref, shi_ref, cw_ref, cb_ref, c0_ref,
                 z_ref, xc_ref, dt_ref, q_ref, k_ref, v_ref, gt_ref, klast_ref, vlast_ref, clast_ref, xe_s,
                 *, tm, keep):
    @pl.when(pl.program_id(1) == 0)
    def _():
        xe_s[0:SUBLANES, :] = c0_ref[...]

    x = x_ref[...]
    hn = _rms(x, g_ref[...]) * (1.0 + sc_ref[...]) + sh_ref[...]
    hb = hn.astype(BF16)
    cw = 1024

    def mm(c0, c1):
        return jnp.dot(hb, w_ref[:, c0:c1], preferred_element_type=F32)

    for c in range(0, CONV_DIM, cw):
        r = mm(_C_XBC + c, _C_XBC + c + cw)
        xe_s[SUBLANES:SUBLANES + tm, c:c + cw] = r
        clast_ref[:, c:c + cw] = r[tm - SUBLANES:tm, :]

    rb, ccw = min(32, tm), 512

    def conv_block(cc):
        cs = slice(cc, cc + ccw)
        bias = jnp.broadcast_to(cb_ref[:, cs], (rb, ccw))
        taps_w = [jnp.broadcast_to(cw_ref[j:j + 1, cs], (rb, ccw)) for j in range(SSM_CONV)]
        for r0 in range(0, tm, rb):
            blk = xe_s[r0:r0 + rb + SUBLANES, cs]
            acc = bias
            for j in range(SSM_CONV):
                back = SSM_CONV - 1 - j
                tap = blk if back == 0 else pltpu.roll(blk, back, 0)
                acc = acc + tap[SUBLANES:SUBLANES + rb] * taps_w[j]
            xc_ref[r0:r0 + rb, cs] = _silu(acc).astype(BF16)

    cos, slo, shi = cos_ref[...], slo_ref[...], shi_ref[...]

    def proj_z(c):
        z_ref[:, c:c + cw] = mm(_C_Z + c, _C_Z + c + cw).astype(BF16)

    def proj_q():
        qf = _rope(mm(_C_Q, _C_K), cos, slo, shi)
        q_ref[...] = (qf * (HEAD_DIM ** -0.5)).astype(BF16)

    def proj_kv():
        kf = _rope(mm(_C_K, _C_V), cos, slo, shi)
        k_ref[...] = kf.astype(BF16)
        klast_ref[...] = kf[tm - keep:tm, :]
        vf = mm(_C_V, _C_G)
        v_ref[...] = vf.astype(BF16)
        vlast_ref[...] = vf[tm - keep:tm, :]

    def proj_g(c):
        gt_ref[:, c:c + cw] = mm(_C_G + c, _C_G + c + cw).astype(BF16)

    def proj_dt():
        dt_ref[...] = jnp.dot(hb, wdt_ref[...], preferred_element_type=F32)

    others = [functools.partial(proj_z, 0), functools.partial(proj_z, cw), proj_q, proj_kv,
              functools.partial(proj_g, 0), functools.partial(proj_g, cw), proj_dt]
    conv_cols = list(range(0, CONV_DIM, ccw))
    for n in range(max(len(others), len(conv_cols))):
        if n < len(others):
            others[n]()
        if n < len(conv_cols):
            conv_block(conv_cols[n])
    xe_s[0:SUBLANES, :] = xe_s[tm:tm + SUBLANES, :]


def _inproj(x, sc, sh, g, w_main, w_dt, cos, slo, shi, conv_w, conv_b, c0, tm):
    b, t, _ = x.shape
    keep = min(WINDOW, t)
    assert t % tm == 0 and tm >= keep and tm % SUBLANES == 0
    row = lambda w: pl.BlockSpec((None, tm, w), lambda bi, i: (bi, i, 0))
    per_b = lambda r, w: pl.BlockSpec((None, r, w), lambda bi, i: (bi, 0, 0))
    tab = pl.BlockSpec((tm, LANES), lambda bi, i: (i, 0))
    sds = lambda w, dt: jax.ShapeDtypeStruct((b, t, w), dt)
    return pl.pallas_call(
        functools.partial(_inproj_body, tm=tm, keep=keep),
        grid=(b, t // tm),
        in_specs=[row(D_MODEL), per_b(1, D_MODEL), per_b(1, D_MODEL), _resident((1, D_MODEL)),
                  _resident(w_main.shape), _resident(w_dt.shape), tab, tab, tab,
                  _resident(conv_w.shape), _resident(conv_b.shape), per_b(SUBLANES, CONV_DIM)],
        out_specs=[row(D_INNER), row(CONV_DIM), row(LANES), row(Q_W), row(KVD_W), row(KVD_W),
                   row(2 * D_MODEL), per_b(keep, KVD_W), per_b(keep, KVD_W), per_b(SUBLANES, CONV_DIM)],
        out_shape=[sds(D_INNER, BF16), sds(CONV_DIM, BF16), sds(LANES, F32), sds(Q_W, BF16),
                   sds(KVD_W, BF16), sds(KVD_W, BF16), sds(2 * D_MODEL, BF16),
                   jax.ShapeDtypeStruct((b, keep, KVD_W), F32), jax.ShapeDtypeStruct((b, keep, KVD_W), F32),
                   jax.ShapeDtypeStruct((b, SUBLANES, CONV_DIM), F32)],
        scratch_shapes=[pltpu.VMEM((SUBLANES + tm, CONV_DIM), F32)],
        compiler_params=_params(VMEM_LIMIT),
        name="inproj",
    )(x, sc, sh, g, w_main, w_dt, cos, slo, shi, conv_w, conv_b, c0)


def _split3(v):
    hi = v.astype(BF16)
    r1 = v - hi.astype(F32)
    mid = r1.astype(BF16)
    lo = (r1 - mid.astype(F32)).astype(BF16)
    return hi, mid, lo


def _softplus(x):
    return jnp.maximum(x, 0.0) + jnp.log1p(jnp.exp(-jnp.abs(x)))


def _expand_row(v, g, lane):
    b = [jnp.broadcast_to(v[:, SSM_HPG * g + r:SSM_HPG * g + r + 1], (1, GROUP_W)) for r in range(SSM_HPG)]
    return jnp.where(lane < 64, b[0], jnp.where(lane < 128, b[1], jnp.where(lane < 192, b[2], b[3])))


def _ssd_body(xc_ref, dt_ref, dtt_ref, dtb_ref, alog_ref, dtbt_ref, alogt_ref, sel_ref, dexp_ref, h0_ref,
              y_ref, hout_ref, h_s, dt_s, a_s, *, ts, L):
    t = pl.program_id(1)

    @pl.when(t == 0)
    def _():
        h_s[...] = h0_ref[...]

    dtv = _softplus(dt_ref[...] + dtb_ref[...])
    dt_s[...] = dtv
    a_s[...] = dtv * (-jnp.exp(alog_ref[...]))
    a_neg_t = -jnp.exp(alogt_ref[...])

    ri = lax.broadcasted_iota(jnp.int32, (L, L), 0)
    ci = lax.broadcasted_iota(jnp.int32, (L, L), 1)
    causal = ci <= ri
    tril = causal.astype(F32).astype(BF16)
    triu = (ri <= ci).astype(F32).astype(BF16)
    e_r = lax.broadcasted_iota(jnp.int32, (LANES, LANES), 0)
    e_c = lax.broadcasted_iota(jnp.int32, (LANES, LANES), 1)
    eye = (e_r == e_c).astype(F32).astype(BF16)
    lane = lax.broadcasted_iota(jnp.int32, (L, GROUP_W), 1)
    lane1 = lax.broadcasted_iota(jnp.int32, (1, GROUP_W), 1)
    n_b = D_INNER
    n_c = D_INNER + SSM_GROUPS * D_STATE

    def chunk(ck, carry):
        rows = pl.ds(pl.multiple_of(ck * L, L), L)
        a_c = a_s[rows, :]
        dt_c = dt_s[rows, :]
        cs = jnp.dot(tril, jnp.concatenate(_split3(a_c), axis=1), preferred_element_type=F32)
        acs = (cs[:, 0:LANES] + cs[:, LANES:2 * LANES]) + cs[:, 2 * LANES:3 * LANES]
        dt_t = _softplus(dtt_ref[ck] + dtbt_ref[...])
        ct = jnp.dot(jnp.concatenate(_split3(dt_t * a_neg_t), axis=0), triu, preferred_element_type=F32)
        acs_t = (ct[0:LANES] + ct[LANES:2 * LANES]) + ct[2 * LANES:3 * LANES]
        acs_last = acs[L - 1:L, :]
        declast = jnp.exp(acs_last)
        ew = jnp.concatenate([jnp.exp(acs), dt_c * jnp.exp(acs_last - acs)], axis=0).astype(BF16)
        ex = jnp.dot(ew, sel_ref[...], preferred_element_type=F32)
        gcols = [slice(GROUP_W * g, GROUP_W * (g + 1)) for g in range(SSM_GROUPS)]
        cbm, b_t, y_x = [], [], []
        for g in range(SSM_GROUPS):
            bb = xc_ref[rows, n_b + D_STATE * g:n_b + D_STATE * (g + 1)]
            cbf = xc_ref[rows, n_c + D_STATE * g:n_c + D_STATE * (g + 1)]
            cbm.append(lax.dot_general(cbf, bb, _NT, preferred_element_type=F32))
            b_t.append(lax.dot_general(eye, bb, _NT, preferred_element_type=F32).astype(BF16))
            y_x.append(jnp.dot(cbf, h_s[g].astype(BF16), preferred_element_type=F32))
        mst, xw = [], []
        for g in range(SSM_GROUPS):
            ms = []
            for r in range(SSM_HPG):
                h = SSM_HPG * g + r
                diff = acs[:, h:h + 1] - acs_t[h:h + 1, :]
                dec = jnp.exp(jnp.where(causal, diff, -jnp.inf))
                ms.append((cbm[g] * dec * dt_t[h:h + 1, :]).astype(BF16))
            mst.append(jnp.concatenate(ms, axis=0))
            xw.append((xc_ref[rows, gcols[g]].astype(F32) * ex[L:2 * L, gcols[g]]).astype(BF16))
        for g in range(SSM_GROUPS):
            xb = xc_ref[rows, gcols[g]]
            rm = jnp.dot(mst[g], xb, preferred_element_type=F32)
            y_in = jnp.where(lane < 64, rm[0:L],
                             jnp.where(lane < 128, rm[L:2 * L], jnp.where(lane < 192, rm[2 * L:3 * L], rm[3 * L:])))
            h_s[g] = (h_s[g] * _expand_row(declast, g, lane1)
                      + jnp.dot(b_t[g], xw[g], preferred_element_type=F32))
            y_ref[rows, gcols[g]] = (y_in + y_x[g] * ex[0:L, gcols[g]]
                                     + xb.astype(F32) * dexp_ref[:, gcols[g]]).astype(BF16)
        return carry

    lax.fori_loop(0, ts // L, chunk, 0)

    @pl.when(t == pl.num_programs(1) - 1)
    def _():
        hout_ref[...] = h_s[...]


def _ssd(xc, dt, dt_t, dtb, alog, dtb_t, alog_t, sel, dexp, h0, ts, L):
    b, t, _ = xc.shape
    assert t % ts == 0 and ts % L == 0
    row = lambda w: pl.BlockSpec((None, ts, w), lambda bi, i: (bi, i, 0))
    st = pl.BlockSpec((None, SSM_GROUPS, D_STATE, GROUP_W), lambda bi, i: (bi, 0, 0, 0))
    return pl.pallas_call(
        functools.partial(_ssd_body, ts=ts, L=L),
        grid=(b, t // ts),
        in_specs=[row(CONV_DIM), row(LANES),
                  pl.BlockSpec((None, ts // L, LANES, L), lambda bi, i: (bi, i, 0, 0)),
                  _resident(dtb.shape), _resident(alog.shape), _resident(dtb_t.shape), _resident(alog_t.shape),
                  _resident(sel.shape), _resident(dexp.shape), st],
        out_specs=[row(D_INNER), st],
        out_shape=[jax.ShapeDtypeStruct((b, t, D_INNER), BF16),
                   jax.ShapeDtypeStruct((b, SSM_GROUPS, D_STATE, GROUP_W), F32)],
        scratch_shapes=[pltpu.VMEM((SSM_GROUPS, D_STATE, GROUP_W), F32),
                        pltpu.VMEM((ts, LANES), F32),
                        pltpu.VMEM((ts, LANES), F32)],
        compiler_params=_params(VMEM_LIMIT),
        name="ssd",
    )(xc, dt, dt_t, dtb, alog, dtb_t, alog_t, sel, dexp, h0)


def _attn_body(sink_ref, q_ref, kc_ref, kp_ref, vc_ref, vp_ref, o_ref, kk_s, vv_s, *, tq, lq, has_cache):
    i = pl.program_id(1)
    kk_s[0:WINDOW, :] = kp_ref[...]
    kk_s[WINDOW:WINDOW + tq, :] = kc_ref[...]
    vv_s[0:WINDOW, :] = vp_ref[...]
    vv_s[WINDOW:WINDOW + tq, :] = vc_ref[...]
    w = WINDOW + lq
    kpos = lax.broadcasted_iota(jnp.int32, (1, w), 1) + (i * tq - WINDOW)
    lo = lax.broadcasted_iota(jnp.int32, (lq, LANES), 1) < HEAD_DIM
    srow = lax.broadcasted_iota(jnp.int32, (Q_PER_KV * lq, 1), 0)
    units = [(j, kh) for j in range(tq // lq) for kh in range(N_KV_HEADS)]

    scores = []
    for j, kh in units:
        qrows = slice(lq * j, lq * (j + 1))
        parts = []
        for half in range(2):
            qt = q_ref[qrows, 2 * LANES * kh + LANES * half:2 * LANES * kh + LANES * (half + 1)]
            zero = jnp.zeros_like(qt)
            parts += [jnp.where(lo, qt, zero), jnp.where(lo, zero, qt)]
        kd = kk_s[lq * j:lq * j + w, LANES * kh:LANES * (kh + 1)]
        scores.append(lax.dot_general(jnp.concatenate(parts, axis=0), kd, _NT, preferred_element_type=F32))

    probs = []
    for (j, kh), s in zip(units, scores):
        if not has_cache:
            s = jnp.where((kpos + lq * j) >= 0, s, jnp.finfo(F32).min)
        sk = [sink_ref[Q_PER_KV * kh + r] for r in range(Q_PER_KV)]
        skc = jnp.where(srow < lq, sk[0], jnp.where(srow < 2 * lq, sk[1], jnp.where(srow < 3 * lq, sk[2], sk[3])))
        m = jnp.maximum(jnp.max(s, axis=-1, keepdims=True), skc)
        e = jnp.exp(s - m)
        den = jnp.sum(e, axis=-1, keepdims=True) + jnp.exp(skc - m)
        probs.append((e / den).astype(BF16))

    for (j, kh), p in zip(units, probs):
        vd = vv_s[lq * j:lq * j + w, LANES * kh:LANES * (kh + 1)]
        o = jnp.dot(p, vd, preferred_element_type=F32)
        qrows = slice(lq * j, lq * (j + 1))
        for half in range(2):
            tile = jnp.where(lo, o[2 * half * lq:(2 * half + 1) * lq], o[(2 * half + 1) * lq:(2 * half + 2) * lq])
            o_ref[qrows, 2 * LANES * kh + LANES * half:2 * LANES * kh + LANES * (half + 1)] = tile.astype(BF16)


def _attn(sinks, q, k, v, k_prev, v_prev, tq, lq, has_cache):
    b, t, _ = q.shape
    assert t % tq == 0 and tq % lq == 0
    row = lambda w: pl.BlockSpec((None, tq, w), lambda bi, i: (bi, i, 0))
    if has_cache:
        prev = pl.BlockSpec((None, WINDOW, KVD_W), lambda bi, i: (bi, 0, 0))
    else:
        assert tq % WINDOW == 0
        nb = tq // WINDOW
        prev = pl.BlockSpec((None, WINDOW, KVD_W), lambda bi, i: (bi, jnp.maximum(i * nb - 1, 0), 0))
    return pl.pallas_call(
        functools.partial(_attn_body, tq=tq, lq=lq, has_cache=has_cache),
        grid=(b, t // tq),
        in_specs=[pl.BlockSpec(memory_space=pltpu.SMEM), row(Q_W), row(KVD_W), prev, row(KVD_W), prev],
        out_specs=row(Q_W),
        out_shape=jax.ShapeDtypeStruct((b, t, Q_W), BF16),
        scratch_shapes=[pltpu.VMEM((WINDOW + tq, KVD_W), BF16), pltpu.VMEM((WINDOW + tq, KVD_W), BF16)],
        compiler_params=_params(VMEM_LIMIT),
        name="attn",
    )(sinks, q, k, k_prev, v, v_prev)


def _post_body(x_ref, yr_ref, z_ref, ya_ref, gt_ref, g1_ref, sh2_ref, sc2_ref, g2_ref, nssm_ref, npost_ref,
               nfpre_ref, nfpost_ref, wso_ref, wao_ref, wo_ref, wfi_ref, fcw_ref, fcb_ref, wfo_ref, f0_ref,
               y_ref, flast_ref, ge_s, *, tm):
    t = pl.program_id(1)

    @pl.when(t == 0)
    def _():
        ge_s[0:SUBLANES, :] = f0_ref[...]

    ys = []
    for g in range(SSM_GROUPS):
        cols = slice(GROUP_W * g, GROUP_W * (g + 1))
        yg = yr_ref[:, cols].astype(F32) * _silu(z_ref[:, cols].astype(F32))
        ys.append(_rms(yg, nssm_ref[:, cols]).astype(BF16))
    a = jnp.dot(jnp.concatenate(ys, axis=1), wso_ref[...], preferred_element_type=F32)
    b = jnp.dot(ya_ref[...], wao_ref[...], preferred_element_type=F32)
    merged = (_sigmoid(gt_ref[:, 0:D_MODEL].astype(F32)) * a
              + _sigmoid(gt_ref[:, D_MODEL:2 * D_MODEL].astype(F32)) * b)
    o = jnp.dot(merged.astype(BF16), wo_ref[...], preferred_element_type=F32)
    x1 = x_ref[...] + g1_ref[...] * _rms(o, npost_ref[...])
    hn = (_rms(x1, nfpre_ref[...]) * (1.0 + sc2_ref[...]) + sh2_ref[...]).astype(BF16)
    ge_s[SUBLANES:SUBLANES + tm, :] = jnp.dot(hn, wfi_ref[:, 0:D_FF], preferred_element_type=F32)
    up = jnp.dot(hn, wfi_ref[:, D_FF:2 * D_FF], preferred_element_type=F32)
    gc = fcb_ref[...]
    for j in range(FFN_CONV):
        off = SUBLANES - (FFN_CONV - 1) + j
        gc = gc + ge_s[off:off + tm, :] * fcw_ref[j:j + 1, :]
    tail = ge_s[tm:tm + SUBLANES, :]
    flast_ref[...] = tail
    ge_s[0:SUBLANES, :] = tail
    f = jnp.dot((_silu(gc) * up).astype(BF16), wfo_ref[...], preferred_element_type=F32)
    y_ref[...] = x1 + g2_ref[...] * _rms(f, nfpost_ref[...])


def _post(x, yr, z, ya, gt, g1, sh2, sc2, g2, nssm, npost, nfpre, nfpost, wso, wao, wo, wfi, fcw, fcb, wfo, f0, tm):
    b, t, _ = x.shape
    assert t % tm == 0 and tm % SUBLANES == 0
    row = lambda w: pl.BlockSpec((None, tm, w), lambda bi, i: (bi, i, 0))
    per_b = lambda r, w: pl.BlockSpec((None, r, w), lambda bi, i: (bi, 0, 0))
    vec = _resident((1, D_MODEL))
    return pl.pallas_call(
        functools.partial(_post_body, tm=tm),
        grid=(b, t // tm),
        in_specs=[row(D_MODEL), row(D_INNER), row(D_INNER), row(Q_W), row(2 * D_MODEL),
                  per_b(1, D_MODEL), per_b(1, D_MODEL), per_b(1, D_MODEL), per_b(1, D_MODEL),
                  _resident(nssm.shape), vec, vec, vec,
                  _resident(wso.shape), _resident(wao.shape), _resident(wo.shape), _resident(wfi.shape),
                  _resident(fcw.shape), _resident(fcb.shape), _resident(wfo.shape), per_b(SUBLANES, D_FF)],
        out_specs=[row(D_MODEL), per_b(SUBLANES, D_FF)],
        out_shape=[jax.ShapeDtypeStruct((b, t, D_MODEL), F32), jax.ShapeDtypeStruct((b, SUBLANES, D_FF), F32)],
        scratch_shapes=[pltpu.VMEM((SUBLANES + tm, D_FF), F32)],
        compiler_params=_params(VMEM_LIMIT),
        name="post",
    )(x, yr, z, ya, gt, g1, sh2, sc2, g2, nssm, npost, nfpre, nfpost, wso, wao, wo, wfi, fcw, fcb, wfo, f0)


def _rope_tables(pos):
    half = ROT_DIM // 2
    inv = ROPE_THETA ** (-jnp.arange(half, dtype=F32) / half)
    ang = pos.astype(F32)[:, None] * inv[None, :]
    cos, sin = jnp.cos(ang), jnp.sin(ang)
    t = pos.shape[0]
    pad = jnp.zeros((t, HEAD_DIM - ROT_DIM), F32)
    zero = jnp.zeros((t, half), F32)
    c64 = jnp.concatenate([cos, cos, pad + 1.0], axis=1)
    lo64 = jnp.concatenate([-sin, zero, pad], axis=1)
    hi64 = jnp.concatenate([zero, sin, pad], axis=1)
    two = lambda a: jnp.concatenate([a, a], axis=1)
    return two(c64), two(lo64), two(hi64)


def _pad_rows(a, rows):
    b, r, w = a.shape
    return jnp.concatenate([jnp.zeros((b, rows - r, w), a.dtype), a], axis=1)


def _double_heads(a):
    lead = a.shape[:-1]
    a4 = a.reshape(lead + (N_KV_HEADS, 1, HEAD_DIM))
    return jnp.broadcast_to(a4, lead + (N_KV_HEADS, 2, HEAD_DIM)).reshape(lead + (KVD_W,))


def _single_heads(a):
    lead = a.shape[:-1]
    return a.reshape(lead + (N_KV_HEADS, 2, HEAD_DIM))[..., 0, :]


def _prep_weights(p, chunk_lens):
    w_in = p['w_in']
    c_dt = D_INNER + CONV_DIM
    c_q = c_dt + SSM_HEADS
    c_k = c_q + Q_W
    c_v = c_k + KV_W
    c_g = c_v + KV_W
    w_main = jnp.concatenate([w_in[:, :c_dt], w_in[:, c_q:c_k], _double_heads(w_in[:, c_k:c_v]),
                              _double_heads(w_in[:, c_v:c_g]), w_in[:, c_g:]], axis=1).astype(BF16)
    w_dt = jnp.pad(w_in[:, c_dt:c_q], ((0, 0), (0, LANES - SSM_HEADS))).astype(BF16)
    padh = lambda a: jnp.pad(a.astype(F32), (0, LANES - SSM_HEADS))
    head_of_col = jnp.arange(D_INNER) // SSM_HEAD_DIM
    sel = (jnp.arange(LANES)[:, None] == head_of_col[None, :]).astype(BF16)
    w = dict(
        w_main=w_main, w_dt=w_dt,
        g_pre=p['norm_mix_pre'].reshape(1, D_MODEL),
        conv_w=p['ssm_conv_w'], conv_b=p['ssm_conv_b'].reshape(1, CONV_DIM),
        dtb=padh(p['ssm_dt_bias']).reshape(1, LANES), alog=padh(p['ssm_a_log']).reshape(1, LANES),
        sel=sel,
        dexp=jnp.repeat(p['ssm_d'].astype(F32), SSM_HEAD_DIM).reshape(1, D_INNER),
        ssm_norm=p['ssm_norm'].reshape(1, D_INNER),
        sinks=p['attn_sinks'].astype(F32),
        wso=p['w_ssm_out'].astype(BF16), wao=p['w_att_out'].astype(BF16), wo=p['w_out'].astype(BF16),
        npost=p['norm_mix_post'].reshape(1, D_MODEL), nfpre=p['norm_ffn_pre'].reshape(1, D_MODEL),
        nfpost=p['norm_ffn_post'].reshape(1, D_MODEL),
        wfi=p['w_ffn_in'].astype(BF16), fcw=p['ffn_conv_w'], fcb=p['ffn_conv_b'].reshape(1, D_FF),
        wfo=p['w_ffn_out'].astype(BF16),
    )
    for L in chunk_lens:
        w['dtb_t%d' % L] = jnp.broadcast_to(padh(p['ssm_dt_bias'])[:, None], (LANES, L))
        w['alog_t%d' % L] = jnp.broadcast_to(padh(p['ssm_a_log'])[:, None], (LANES, L))
    return w


def _layer(x, mod, pos, conv_ctx, h0, kv_ctx, ffn_ctx, w, tiles):
    b, t, _ = x.shape
    tm_in, ts, L, tq, lq, tm_post = tiles
    sh1, sc1, g1, sh2, sc2, g2 = [m.reshape(b, 1, D_MODEL) for m in jnp.split(mod, 6, axis=-1)]
    cos, slo, shi = _rope_tables(pos)
    z, xc, dt, q, kd, vd, gt, k_last, v_last, c_last = _inproj(
        x, sc1, sh1, w['g_pre'], w['w_main'], w['w_dt'], cos, slo, shi, w['conv_w'], w['conv_b'],
        _pad_rows(conv_ctx.astype(F32), SUBLANES), tm_in)
    conv_new = c_last[:, SUBLANES - (SSM_CONV - 1):, :]

    h0_t = h0.astype(F32).reshape(b, SSM_GROUPS, SSM_HPG, SSM_HEAD_DIM, D_STATE)
    h0_t = jnp.transpose(h0_t, (0, 1, 4, 2, 3)).reshape(b, SSM_GROUPS, D_STATE, GROUP_W)
    dt_t = jnp.transpose(dt.reshape(b, t // L, L, LANES), (0, 1, 3, 2))
    y_raw, h_t = _ssd(xc, dt, dt_t, w['dtb'], w['alog'], w['dtb_t%d' % L], w['alog_t%d' % L], w['sel'],
                      w['dexp'], h0_t, ts, L)
    h_new = jnp.transpose(h_t.reshape(b, SSM_GROUPS, D_STATE, SSM_HPG, SSM_HEAD_DIM), (0, 1, 3, 4, 2))
    h_new = h_new.reshape(b, SSM_HEADS, SSM_HEAD_DIM, D_STATE)

    if kv_ctx is None:
        y_att = _attn(w['sinks'], q, kd, vd, kd, vd, tq, lq, False)
        k_new, v_new = _single_heads(k_last), _single_heads(v_last)
    else:
        ck, cv = kv_ctx
        rows = ck.shape[1]
        y_att = _attn(w['sinks'], q, kd, vd, _double_heads(ck.reshape(b, rows, KV_W)).astype(BF16),
                      _double_heads(cv.reshape(b, rows, KV_W)).astype(BF16), tq, lq, True)
        k_new = jnp.concatenate([ck, _single_heads(k_last)], axis=1)[:, -rows:]
        v_new = jnp.concatenate([cv, _single_heads(v_last)], axis=1)[:, -rows:]

    y, f_last = _post(x, y_raw, z, y_att, gt, g1, sh2, sc2, g2, w['ssm_norm'], w['npost'], w['nfpre'],
                      w['nfpost'], w['wso'], w['wao'], w['wo'], w['wfi'], w['fcw'], w['fcb'], w['wfo'],
                      _pad_rows(ffn_ctx.astype(F32), SUBLANES), tm_post)
    ffn_new = f_last[:, SUBLANES - (FFN_CONV - 1):, :]
    return y, (h_new, conv_new, k_new, v_new, ffn_new)


def _tiles(t):
    L = min(CHUNK, t)
    return (min(256, t), min(256, t), L, min(256, t), L, min(256, t))


def kernel(x_prompt, x_sample, cache_k, cache_v, state_ssm, state_conv, state_ffn_conv, c_prompt, c_sample,
           w_ada, b_ada, norm_mix_pre, norm_mix_post, w_in, ssm_conv_w, ssm_conv_b, ssm_dt_bias, ssm_a_log,
           ssm_d, ssm_norm, attn_sinks, w_ssm_out, w_att_out, w_out, norm_ffn_pre, norm_ffn_post,
           w_ffn_in, ffn_conv_w, ffn_conv_b, w_ffn_out):
    depth = w_in.shape[0]
    bp, tp, _ = x_prompt.shape
    bs, tsamp, _ = x_sample.shape
    pos_p = jnp.arange(tp)
    pos_s = PAST_LEN + jnp.arange(tsamp)
    tiles_p, tiles_s = _tiles(tp), _tiles(tsamp)
    yp, ys = x_prompt, x_sample
    st_p, st_s = [], []
    for l in range(depth):
        p = {
            'w_ada': w_ada[l], 'b_ada': b_ada[l], 'norm_mix_pre': norm_mix_pre[l], 'norm_mix_post': norm_mix_post[l],
            'w_in': w_in[l], 'ssm_conv_w': ssm_conv_w[l], 'ssm_conv_b': ssm_conv_b[l], 'ssm_dt_bias': ssm_dt_bias[l],
            'ssm_a_log': ssm_a_log[l], 'ssm_d': ssm_d[l], 'ssm_norm': ssm_norm[l], 'attn_sinks': attn_sinks[l],
            'w_ssm_out': w_ssm_out[l], 'w_att_out': w_att_out[l], 'w_out': w_out[l],
            'norm_ffn_pre': norm_ffn_pre[l], 'norm_ffn_post': norm_ffn_post[l], 'w_ffn_in': w_ffn_in[l],
            'ffn_conv_w': ffn_conv_w[l], 'ffn_conv_b': ffn_conv_b[l], 'w_ffn_out': w_ffn_out[l],
        }
        w = _prep_weights(p, {tiles_p[2], tiles_s[2]})
        mod = _ada(jnp.concatenate([c_prompt, c_sample], axis=0), p['w_ada'], p['b_ada'])
        conv0 = jnp.zeros((bp, SSM_CONV - 1, CONV_DIM), F32)
        h0 = jnp.zeros((bp, SSM_HEADS, SSM_HEAD_DIM, D_STATE), F32)
        ffn0 = jnp.zeros((bp, FFN_CONV - 1, D_FF), F32)
        yp, sp = _layer(yp, mod[:bp], pos_p, conv0, h0, None, ffn0, w, tiles_p)
        ys, ss = _layer(ys, mod[bp:], pos_s, state_conv[l], state_ssm[l], (cache_k[l], cache_v[l]),
                        state_ffn_conv[l], w, tiles_s)
        st_p.append(sp)
        st_s.append(ss)
    ssm_p, conv_p, k_p, v_p, ffn_p = [jnp.stack(t) for t in zip(*st_p)]
    ssm_s, conv_s, k_s, v_s, ffn_s = [jnp.stack(t) for t in zip(*st_s)]
    return (yp, ys, ssm_p, conv_p, k_p, v_p, ffn_p, ssm_s, conv_s, k_s, v_s, ffn_s)
```
